```python
import math
import jax, jax.numpy as jnp
from jax import lax
import numpy as np

D_MODEL = 1024
BATCH = 1
SEQ = 16384
DEPTH = 1
DEC_BATCH = 128
DEC_SEQ = 8
PAST_LEN = 16384
PAGE_SIZE = 128

N_HEADS = 8
KV_HEADS = 2
HEAD_DIM = D_MODEL // 16
GROUP = N_HEADS // KV_HEADS
ATT_W = N_HEADS * HEAD_DIM
KV_W = KV_HEADS * HEAD_DIM
WINDOW = 128
BLOCK = 128
N_BUCKETS = 32
MAX_DISTANCE = 128
MAX_EXACT = N_BUCKETS // 2
GLA_HEADS = 4
GLA_DK = D_MODEL // 2
GLA_DV = D_MODEL
GLA_HK = GLA_DK // GLA_HEADS
GLA_HV = GLA_DV // GLA_HEADS
GLA_RANK = 16
GATE_NORMALIZER = 16.0
GLA_CHUNK = 64
SPLITS = (ATT_W, KV_W, KV_W, ATT_W, GLA_DK, GLA_DK, GLA_DV, GLA_DV, GLA_RANK, D_MODEL, D_MODEL)
N_IN = sum(SPLITS)
SPLIT_POINTS = tuple(int(s) for s in np.cumsum(SPLITS)[:-1])
DN_ALPHA = (2.0 * DEPTH) ** 0.25
DN_BETA = (8.0 * DEPTH) ** -0.25
EPS = 1e-5

kernel_name = "hybrid_swa_gla_gated_step"


def layer_norm(x, g, b):
    xf = x.astype(jnp.float32)
    mu = jnp.mean(xf, -1, keepdims=True)
    var = jnp.mean(jnp.square(xf - mu), -1, keepdims=True)
    return ((xf - mu) * lax.rsqrt(var + EPS) * g + b).astype(x.dtype)


def t5_bucket(dist):
    n = jnp.maximum(dist, 0)
    nf = jnp.maximum(n, 1).astype(jnp.float32)
    large = MAX_EXACT + (jnp.log(nf / MAX_EXACT) / math.log(MAX_DISTANCE / MAX_EXACT)
                         * (N_BUCKETS - MAX_EXACT)).astype(jnp.int32)
    large = jnp.minimum(large, N_BUCKETS - 1)
    return jnp.where(n < MAX_EXACT, n, large)


def rel_bias(rel_table, dist):
    b = rel_table[t5_bucket(dist)].astype(jnp.float32)
    return jnp.transpose(b, (2, 0, 1)).reshape(KV_HEADS, GROUP, dist.shape[0], dist.shape[1])


def sink_softmax(logits, mask, sink):
    logits = jnp.where(mask, logits, -1e30)
    m = jnp.maximum(jnp.max(logits, -1, keepdims=True), sink)
    p = jnp.exp(logits - m)
    return p / (jnp.sum(p, -1, keepdims=True) + jnp.exp(sink - m))


def swa_prompt(q, k, v, sink, rel_table):
    B, T = q.shape[0], q.shape[1]
    nb = T // BLOCK
    qb = q.reshape(B, nb, BLOCK, KV_HEADS, GROUP, HEAD_DIM)
    kb = k.reshape(B, nb, BLOCK, KV_HEADS, HEAD_DIM)
    vb = v.reshape(B, nb, BLOCK, KV_HEADS, HEAD_DIM)
    pad = ((0, 0), (1, 0), (0, 0), (0, 0), (0, 0))
    kband = jnp.concatenate([jnp.pad(kb, pad)[:, :-1], kb], axis=2)
    vband = jnp.concatenate([jnp.pad(vb, pad)[:, :-1], vb], axis=2)
    s = jnp.einsum('bnqhgd,bnkhd->bnhgqk', qb, kband).astype(jnp.float32) * (HEAD_DIM ** -0.5)
    dist = jnp.arange(BLOCK)[:, None] + BLOCK - jnp.arange(2 * BLOCK)[None, :]
    s = s + rel_bias(rel_table, dist)
    in_win = (dist >= 0) & (dist <= WINDOW)
    has_prev = (jnp.arange(nb)[:, None] > 0) | (jnp.arange(2 * BLOCK)[None, :] >= BLOCK)
    mask = (in_win[None] & has_prev[:, None, :])[None, :, None, None]
    p = sink_softmax(s, mask, sink.astype(jnp.float32).reshape(KV_HEADS, GROUP, 1, 1))
    o = jnp.einsum('bnhgqk,bnkhd->bnqhgd', p.astype(v.dtype), vband)
    return o.reshape(B, T, ATT_W)


def swa_sample(q, k_new, v_new, cache_k, cache_v, sink, rel_table):
    B, T = q.shape[0], q.shape[1]
    kall = jnp.concatenate([cache_k.astype(k_new.dtype), k_new], axis=1)
    vall = jnp.concatenate([cache_v.astype(v_new.dtype), v_new], axis=1)
    s = jnp.einsum('bqhgd,bkhd->bhgqk', q, kall).astype(jnp.float32) * (HEAD_DIM ** -0.5)
    dist = jnp.arange(T)[:, None] + WINDOW - jnp.arange(WINDOW + T)[None, :]
    s = s + rel_bias(rel_table, dist)
    mask = (dist >= 0) & (dist <= WINDOW)
    p = sink_softmax(s, mask, sink.astype(jnp.float32).reshape(KV_HEADS, GROUP, 1, 1))
    o = jnp.einsum('bhgqk,bkhd->bqhgd', p.astype(vall.dtype), vall).reshape(B, T, ATT_W)
    return o, kall[:, T:], vall[:, T:]


def gla(q, k, v, log_a, s0):
    B, T = q.shape[0], q.shape[1]
    c = math.gcd(GLA_CHUNK, T)
    nc = T // c

    def to_chunks(a):
        return jnp.moveaxis(a.reshape(B, nc, c, *a.shape[2:]), 1, 0).astype(jnp.float32)

    causal = jnp.tril(jnp.ones((c, c), dtype=bool))

    def step(S, xs):
        qc, kc, vc, gc = xs
        b = jnp.cumsum(gc, axis=1)
        b_last = b[:, -1]
        qt = qc * jnp.exp(b)
        kt = kc * jnp.exp(-b)
        kend = kc * jnp.exp(b_last[:, None] - b)
        o_inter = jnp.einsum('bthk,bhkv->bthv', qt, S)
        A = jnp.where(causal, jnp.einsum('bthk,bshk->bhts', qt, kt), 0.0)
        o_intra = jnp.einsum('bhts,bshv->bthv', A, vc)
        S = jnp.exp(b_last)[..., None] * S + jnp.einsum('bshk,bshv->bhkv', kend, vc)
        return S, o_inter + o_intra

    S, o = lax.scan(step, s0.astype(jnp.float32), (to_chunks(q), to_chunks(k), to_chunks(v), to_chunks(log_a)))
    o = jnp.moveaxis(o, 0, 1).reshape(B, T, GLA_HEADS, GLA_HV)
    return o, S


def hybrid_layer(x, cache_k, cache_v, s0, rel_table, w_in, w_gk_up, b_gk, attn_sink,
                 gla_norm_w, w_pa, w_pg, w_o, ln_g, ln_b, prompt):
    B, T, _ = x.shape
    z = jnp.einsum('btd,dn->btn', x, w_in)
    q, k, v, g_att, gq, gk, gv, g_gla, g_lr, r_att, r_gla = jnp.split(z, SPLIT_POINTS, axis=-1)
    q = q.reshape(B, T, KV_HEADS, GROUP, HEAD_DIM)
    k = k.reshape(B, T, KV_HEADS, HEAD_DIM)
    v = v.reshape(B, T, KV_HEADS, HEAD_DIM)
    if prompt:
        o_att = swa_prompt(q, k, v, attn_sink, rel_table)
        new_k, new_v = k[:, T - WINDOW:], v[:, T - WINDOW:]
    else:
        o_att, new_k, new_v = swa_sample(q, k, v, cache_k, cache_v, attn_sink, rel_table)
    log_a = jax.nn.log_sigmoid((jnp.einsum('btr,rk->btk', g_lr, w_gk_up) + b_gk).astype(jnp.float32)) / GATE_NORMALIZER
    o_gla, S = gla((gq * (GLA_HK ** -0.5)).reshape(B, T, GLA_HEADS, GLA_HK),
                   gk.reshape(B, T, GLA_HEADS, GLA_HK),
                   gv.reshape(B, T, GLA_HEADS, GLA_HV),
                   log_a.reshape(B, T, GLA_HEADS, GLA_HK), s0)
    o_gla = o_gla * lax.rsqrt(jnp.mean(jnp.square(o_gla), -1, keepdims=True) + EPS) * gla_norm_w
    o_gla = o_gla.reshape(B, T, GLA_DV).astype(x.dtype)
    h_att = jnp.einsum('bta,ad->btd', o_att * jax.nn.silu(g_att), w_pa)
    h_gla = jnp.einsum('btv,vd->btd', o_gla * jax.nn.silu(g_gla), w_pg)
    merged = jax.nn.sigmoid(r_att) * h_att + jax.nn.sigmoid(r_gla) * h_gla
    y = jnp.einsum('btd,de->bte', merged, w_o)
    x = layer_norm(DN_ALPHA * x + y, ln_g, ln_b)
    return x, new_k, new_v, S


def setup_inputs(seed: int = 0) -> dict:
    key = jax.random.key(seed)
    ks = jax.random.split(key, 20)
    f32 = jnp.float32
    nrm = lambda k, shape, s: jax.random.normal(k, shape, f32) * s
    return {
        "x_prompt": nrm(ks[0], (BATCH, SEQ, D_MODEL), 1.0),
        "x_sample": nrm(ks[1], (DEC_BATCH, DEC_SEQ, D_MODEL), 1.0),
        "cache_k": nrm(ks[2], (DEPTH, DEC_BATCH, WINDOW, KV_HEADS, HEAD_DIM), 1.0),
        "cache_v": nrm(ks[3], (DEPTH, DEC_BATCH, WINDOW, KV_HEADS, HEAD_DIM), 1.0),
        "state_gla": nrm(ks[4], (DEPTH, DEC_BATCH, GLA_HEADS, GLA_HK, GLA_HV), 0.3),
        "rel_bias_table": nrm(ks[5], (N_BUCKETS, N_HEADS), 0.5),
        "w_in": nrm(ks[6], (DEPTH, D_MODEL, N_IN), D_MODEL ** -0.5),
        "w_gk_up": nrm(ks[7], (DEPTH, GLA_RANK, GLA_DK), GLA_RANK ** -0.5),
        "b_gk": nrm(ks[8], (DEPTH, GLA_DK), 0.1),
        "attn_sink": nrm(ks[9], (DEPTH, N_HEADS), 0.5),
        "gla_norm_w": 1.0 + nrm(ks[10], (DEPTH, GLA_HV), 0.05),
        "w_pa": nrm(ks[11], (DEPTH, ATT_W, D_MODEL), ATT_W ** -0.5 * DN_BETA),
        "w_pg": nrm(ks[12], (DEPTH, GLA_DV, D_MODEL), GLA_DV ** -0.5 * DN_BETA),
        "w_o": nrm(ks[13], (DEPTH, D_MODEL, D_MODEL), D_MODEL ** -0.5 * DN_BETA),
        "ln_g": 1.0 + nrm(ks[14], (DEPTH, D_MODEL), 0.05),
        "ln_b": nrm(ks[15], (DEPTH, D_MODEL), 0.02),
    }


def reference(x_prompt, x_sample, cache_k, cache_v, state_gla, rel_bias_table, w_in, w_gk_up, b_gk,
              attn_sink, gla_norm_w, w_pa, w_pg, w_o, ln_g, ln_b):
    xp, xs = x_prompt, x_sample
    kp, vp, sp, ksm, vsm, ssm = [], [], [], [], [], []
    for l in range(DEPTH):
        s0 = jnp.zeros((xp.shape[0], GLA_HEADS, GLA_HK, GLA_HV), jnp.float32)
        xp, nk, nv, S = hybrid_layer(xp, None, None, s0, rel_bias_table, w_in[l], w_gk_up[l], b_gk[l],
                                     attn_sink[l], gla_norm_w[l], w_pa[l], w_pg[l], w_o[l], ln_g[l], ln_b[l], True)
        kp.append(nk); vp.append(nv); sp.append(S)
        xs, nk, nv, S = hybrid_layer(xs, cache_k[l], cache_v[l], state_gla[l], rel_bias_table, w_in[l], w_gk_up[l],
                                     b_gk[l], attn_sink[l], gla_norm_w[l], w_pa[l], w_pg[l], w_o[l], ln_g[l],
                                     ln_b[l], False)
        ksm.append(nk); vsm.append(nv); ssm.append(S)
    return (xp, xs, jnp.stack(kp), jnp.stack(vp), jnp.stack(sp), jnp.stack(ksm), jnp.stack(vsm), jnp.stack(ssm))
```

```python
import functools
import math

import numpy as np
import jax
import jax.numpy as jnp
from jax import lax
from jax.experimental import pallas as pl
from jax.experimental.pallas import tpu as pltpu

D_MODEL = 1024
N_HEADS = 8
KV_HEADS = 2
HEAD_DIM = 64
GROUP = N_HEADS // KV_HEADS
ATT_W = N_HEADS * HEAD_DIM
KV_W = KV_HEADS * HEAD_DIM
WINDOW = 128
N_BUCKETS = 32
MAX_DISTANCE = 128
MAX_EXACT = N_BUCKETS // 2
GLA_HEADS = 4
GLA_DK = D_MODEL // 2
GLA_DV = D_MODEL
GLA_HK = GLA_DK // GLA_HEADS
GLA_HV = GLA_DV // GLA_HEADS
GLA_RANK = 16
GATE_NORMALIZER = 16.0
GLA_CHUNK = 64
EPS = 1e-5
NEG = -1e30

LANES = 128
TB = 256
NCHUNK = TB // GLA_CHUNK
SB = 8
VMEM_LIMIT = 56 * 1024 * 1024

C_Q = 0
C_K = C_Q + ATT_W
C_V = C_K + KV_W
C_GA = C_V + KV_W
C_ATT_END = C_GA + ATT_W
C_GQ = C_ATT_END
C_GK = C_GQ + GLA_DK
C_GV = C_GK + GLA_DK
C_GG = C_GV + GLA_DV
C_GLA_END = C_GG + GLA_DV
C_RA = C_GLA_END
C_RG = C_RA + D_MODEL
C_R_END = C_RG + D_MODEL
C_LR = C_R_END
N_PACK = C_LR + LANES

BF = jnp.bfloat16
F32 = jnp.float32


def _t5_bucket_np(dist):
    n = np.maximum(dist, 0)
    nf = np.maximum(n, 1).astype(np.float64)
    large = MAX_EXACT + (np.log(nf / MAX_EXACT) / math.log(MAX_DISTANCE / MAX_EXACT)
                         * (N_BUCKETS - MAX_EXACT)).astype(np.int32)
    large = np.minimum(large, N_BUCKETS - 1)
    return np.where(n < MAX_EXACT, n, large).astype(np.int32)


def _att_perm():
    new = np.arange(ATT_W)
    g, h, d = new // LANES, (new % LANES) // HEAD_DIM, new % HEAD_DIM
    return h * (GROUP * HEAD_DIM) + g * HEAD_DIM + d


def _dot(a, b):
    return jnp.dot(a, b, preferred_element_type=F32)


def _dot_nt(a, b):
    return lax.dot_general(a, b, (((1,), (1,)), ((), ())), preferred_element_type=F32)


def _dot_tn(a, b):
    return lax.dot_general(a, b, (((0,), (0,)), ((), ())), preferred_element_type=F32)


def _log_sigmoid(x):
    return jnp.minimum(x, 0.0) - jnp.log1p(jnp.exp(-jnp.abs(x)))


def _silu(x):
    return x * jax.nn.sigmoid(x)


def _group_cumsum(x, group):
    row = lax.broadcasted_iota(jnp.int32, x.shape, 0) & (group - 1)
    s = 1
    while s < group:
        x = x + jnp.where(row >= s, pltpu.roll(x, s, 0), 0.0)
        s *= 2
    return x


def _row_bcast(x, r, n):
    return jnp.broadcast_to(x[r:r + 1, :], (n, x.shape[1]))


def _col_from_row(row, eye):
    n = row.shape[1]
    return jnp.sum(jnp.where(eye, jnp.broadcast_to(row, (n, n)), 0.0), axis=1, keepdims=True)


def _kv_variants(kk, vv):
    lo = lax.broadcasted_iota(jnp.int32, kk.shape, 1) < HEAD_DIM
    sw = pltpu.roll(kk, HEAD_DIM, 1)
    z = jnp.zeros_like(kk)
    kvar = ((jnp.where(lo, kk, z).astype(BF), jnp.where(lo, z, sw).astype(BF)),
            (jnp.where(lo, sw, z).astype(BF), jnp.where(lo, z, kk).astype(BF)))
    return kvar, vv.astype(BF)


def _build_bias(bucket, table_ref, mask, head, acc0=None):
    acc = jnp.zeros(bucket.shape, F32) if acc0 is None else acc0
    for b in range(N_BUCKETS):
        acc = jnp.where(bucket == b, table_ref[b * N_HEADS + head], acc)
    return jnp.where(mask, acc, NEG)


def _softmax_pv(s, sink, vcat):
    m = jnp.maximum(jnp.max(s, axis=-1, keepdims=True), sink)
    p = jnp.exp(s - m)
    l = jnp.sum(p, axis=-1, keepdims=True) + jnp.exp(sink - m)
    return _dot(p.astype(BF), vcat) / l


def _gla_prep(zl, wup_ref, bgk_ref, group):
    pre = _dot(zl.astype(BF), wup_ref[...]) + bgk_ref[...]
    g = _log_sigmoid(pre) / GATE_NORMALIZER
    return _group_cumsum(g, group)


def _out_stage(x, r_att, r_gla, o_att, g_att, o_gla, g_gla, wpa_ref, wpg_ref, wo_ref, lng_ref, lnb_ref, dn_alpha):
    h_att = _dot((o_att * _silu(g_att)).astype(BF), wpa_ref[...])
    h_gla = _dot((o_gla * _silu(g_gla)).astype(BF), wpg_ref[...])
    merged = jax.nn.sigmoid(r_att) * h_att + jax.nn.sigmoid(r_gla) * h_gla
    y = _dot(merged.astype(BF), wo_ref[...])
    xf = dn_alpha * x + y
    mu = jnp.mean(xf, axis=-1, keepdims=True)
    xc = xf - mu
    var = jnp.mean(xc * xc, axis=-1, keepdims=True)
    return xc * lax.rsqrt(var + EPS) * lng_ref[...] + lnb_ref[...]


def _gla_norm(o, gnw_ref):
    return o * lax.rsqrt(jnp.mean(o * o, axis=-1, keepdims=True) + EPS) * gnw_ref[...]


def _prompt_kernel(x_ref, w_ref, wup_ref, bgk_ref, wpa_ref, wpg_ref, wo_ref, gnw_ref, lng_ref, lnb_ref,
                   bucket_ref, table_ref, sink_ref,
                   y_ref, ko_ref, vo_ref, s_ref,
                   bias_sc, kprev_sc, vprev_sc, *, dn_alpha):
    step = pl.program_id(0)
    nsteps = pl.num_programs(0)

    @pl.when(step == 0)
    def _init():
        bucket = bucket_ref[...]
        qi = lax.broadcasted_iota(jnp.int32, bucket.shape, 0)
        ki = lax.broadcasted_iota(jnp.int32, bucket.shape, 1)
        dist = qi + WINDOW - ki
        in_win = (dist >= 0) & (dist <= WINDOW)
        for n in range(N_HEADS):
            full = _build_bias(bucket, table_ref, in_win, n)
            bias_sc[0, n] = full
            bias_sc[1, n] = jnp.where(ki >= WINDOW, full, NEG)
        kprev_sc[...] = jnp.zeros(kprev_sc.shape, BF)
        vprev_sc[...] = jnp.zeros(vprev_sc.shape, BF)
        s_ref[...] = jnp.zeros(s_ref.shape, F32)

    x = x_ref[...]
    xb = x.astype(BF)

    za = _dot(xb, w_ref[:, C_Q:C_ATT_END])
    qb = za[:, C_Q:C_K].astype(BF)
    lo = lax.broadcasted_iota(jnp.int32, (WINDOW, LANES), 1) < HEAD_DIM
    k_prev = tuple(tuple(kprev_sc[2 * h + p] for p in range(2)) for h in range(KV_HEADS))
    v_prev = vprev_sc[...]
    o_rows = []
    for blk in range(TB // WINDOW):
        r0 = blk * WINDOW
        k_cur, v_cur = _kv_variants(za[r0:r0 + WINDOW, C_K:C_V], za[r0:r0 + WINDOW, C_V:C_GA])
        vcat = jnp.concatenate([v_prev, v_cur], axis=0)
        first = jnp.where(step == 0, 1, 0) if blk == 0 else 0
        outs = {}
        for h in range(KV_HEADS):
            qpair = jnp.concatenate([qb[r0:r0 + WINDOW, (2 * h) * LANES:(2 * h + 1) * LANES],
                                     qb[r0:r0 + WINDOW, (2 * h + 1) * LANES:(2 * h + 2) * LANES]], axis=0)
            for p in range(2):
                kcat = jnp.concatenate([k_prev[h][p], k_cur[h][p]], axis=0)
                s = _dot_nt(qpair, kcat)
                for half in range(2):
                    g = 2 * half + p
                    n = h * GROUP + g
                    sh = s[half * WINDOW:(half + 1) * WINDOW] + bias_sc[first, n]
                    outs[(h, g)] = _softmax_pv(sh, sink_ref[n], vcat)
        tiles = [jnp.where(lo, outs[(0, g)], outs[(1, g)]) for g in range(GROUP)]
        o_rows.append(jnp.concatenate(tiles, axis=1))
        k_prev, v_prev = k_cur, v_cur
    for h in range(KV_HEADS):
        for p in range(2):
            kprev_sc[2 * h + p] = k_prev[h][p]
    vprev_sc[...] = v_prev
    o_att = jnp.concatenate(o_rows, axis=0)
    g_att = za[:, C_GA:C_ATT_END]

    @pl.when(step == nsteps - 1)
    def _emit_kv():
        ko_ref[...] = za[TB - WINDOW:, C_K:C_V]
        vo_ref[...] = za[TB - WINDOW:, C_V:C_GA]

    zg = _dot(xb, w_ref[:, C_GQ:C_GLA_END])
    b = _gla_prep(_dot(xb, w_ref[:, C_LR:C_LR + LANES]), wup_ref, bgk_ref, GLA_CHUNK)
    bl = [b[c * GLA_CHUNK + GLA_CHUNK - 1:(c + 1) * GLA_CHUNK, :] for c in range(NCHUNK)]

    def rows_of(parts):
        return jnp.concatenate([jnp.broadcast_to(p_, (GLA_CHUNK, GLA_DK)) for p_ in parts], axis=0)

    def span(lo_c, hi_c):
        tot = jnp.zeros((1, GLA_DK), F32)
        for i in range(lo_c, hi_c):
            tot = tot + bl[i]
        return tot

    eb = jnp.exp(b)
    enb = jnp.exp(-b)
    ekend = jnp.exp(rows_of(bl) - b)
    e_pre = rows_of([jnp.exp(span(0, c)) for c in range(NCHUNK)])
    e_tail = rows_of([jnp.exp(span(c + 1, NCHUNK)) for c in range(NCHUNK)])
    e_tot = jnp.exp(span(0, NCHUNK))
    gq = zg[:, 0:GLA_DK] * (GLA_HK ** -0.5)
    gk = zg[:, GLA_DK:2 * GLA_DK]
    qt = gq * eb
    kt = gk * enb
    kend = gk * ekend
    qp = (qt * e_pre).astype(BF)
    kfull = (kend * e_tail).astype(BF)
    qtb = qt.astype(BF)
    ktb = kt.astype(BF)
    kmid = {}
    for c in range(NCHUNK):
        for j in range(c):
            kj = kend[j * GLA_CHUNK:(j + 1) * GLA_CHUNK]
            kmid[(c, j)] = (kj if c == j + 1 else kj * jnp.exp(span(j + 1, c))).astype(BF)
    tril = (lax.broadcasted_iota(jnp.int32, (TB, TB), 0) >= lax.broadcasted_iota(jnp.int32, (TB, TB), 1))
    eye = (lax.broadcasted_iota(jnp.int32, (GLA_HK, GLA_HK), 0) == lax.broadcasted_iota(jnp.int32, (GLA_HK, GLA_HK), 1))
    zero_k = jnp.zeros((GLA_CHUNK, GLA_HK), BF)
    o_heads = []
    for hh in range(GLA_HEADS):
        ks = slice(hh * GLA_HK, (hh + 1) * GLA_HK)
        vh = zg[:, 2 * GLA_DK + hh * GLA_HV:2 * GLA_DK + (hh + 1) * GLA_HV].astype(BF)
        s_old = s_ref[hh]
        o = _dot(qp[:, ks], s_old.astype(BF))
        a_rows = []
        for c in range(NCHUNK):
            keys = [kmid[(c, j)][:, ks] for j in range(c)] + [ktb[c * GLA_CHUNK:(c + 1) * GLA_CHUNK, ks]]
            keys += [zero_k] * (NCHUNK - 1 - c)
            a_rows.append(_dot_nt(qtb[c * GLA_CHUNK:(c + 1) * GLA_CHUNK, ks], jnp.concatenate(keys, axis=0)))
        a = jnp.where(tril, jnp.concatenate(a_rows, axis=0), 0.0).astype(BF)
        o = o + _dot(a, vh)
        s_ref[hh] = _col_from_row(e_tot[:, ks], eye) * s_old + _dot_tn(kfull[:, ks], vh)
        o_heads.append(_gla_norm(o, gnw_ref))
    o_gla = jnp.concatenate(o_heads, axis=1)
    g_gla = zg[:, 2 * GLA_DK + GLA_DV:]

    zr = _dot(xb, w_ref[:, C_RA:C_R_END])
    y_ref[...] = _out_stage(x, zr[:, :D_MODEL], zr[:, D_MODEL:], o_att, g_att, o_gla, g_gla,
                            wpa_ref, wpg_ref, wo_ref, lng_ref, lnb_ref, dn_alpha)


def _sample_kernel(x_ref, w_ref, wup_ref, bgk_ref, wpa_ref, wpg_ref, wo_ref, gnw_ref, lng_ref, lnb_ref,
                   bucket_ref, table_ref, sink_ref, ck_ref, cv_ref, st_ref,
                   y_ref, nk_ref, nv_ref, ns_ref,
                   bias_sc, za_sc, zg_sc, b_sc, oatt_sc, ogla_sc, *, dn_alpha, t_new):
    i = pl.program_id(0)
    j = pl.program_id(1)
    nsub = pl.num_programs(1)
    rows = SB * t_new
    nkeys = 2 * WINDOW

    @pl.when((i == 0) & (j == 0))
    def _init():
        bucket = bucket_ref[...]
        ti = lax.broadcasted_iota(jnp.int32, bucket.shape, 0) & (t_new - 1)
        upper = lax.broadcasted_iota(jnp.int32, bucket.shape, 0) >= t_new
        ki = lax.broadcasted_iota(jnp.int32, bucket.shape, 1)
        dist = ti + WINDOW - ki
        in_win = (dist >= 0) & (dist <= WINDOW)
        for h in range(KV_HEADS):
            for p in range(2):
                a0 = _build_bias(bucket, table_ref, in_win, h * GROUP + p)
                a1 = _build_bias(bucket, table_ref, in_win, h * GROUP + 2 + p)
                bias_sc[2 * h + p] = jnp.where(upper, a1, a0)

    @pl.when(j == 0)
    def _project():
        xb = x_ref[...].astype(BF)
        za_sc[...] = _dot(xb, w_ref[:, C_Q:C_ATT_END])
        zg_sc[...] = _dot(xb, w_ref[:, C_GQ:C_GLA_END])
        b_sc[...] = _gla_prep(_dot(xb, w_ref[:, C_LR:C_LR + LANES]), wup_ref, bgk_ref, t_new)

    r0 = pl.multiple_of(j * rows, rows)
    za = za_sc[pl.ds(r0, rows), :]
    zg = zg_sc[pl.ds(r0, rows), :]
    b = b_sc[pl.ds(r0, rows), :]

    lo = lax.broadcasted_iota(jnp.int32, (t_new, LANES), 1) < HEAD_DIM
    upper1 = lax.broadcasted_iota(jnp.int32, (2 * t_new, 1), 0) >= t_new
    k_new = za[:, C_K:C_V]
    v_new = za[:, C_V:C_GA]
    nk_ref[:, 0:WINDOW - t_new, :] = ck_ref[:, t_new:WINDOW, :]
    nv_ref[:, 0:WINDOW - t_new, :] = cv_ref[:, t_new:WINDOW, :]
    nk_ref[:, WINDOW - t_new:WINDOW, :] = k_new.reshape(SB, t_new, KV_W)
    nv_ref[:, WINDOW - t_new:WINDOW, :] = v_new.reshape(SB, t_new, KV_W)
    pad = jnp.zeros((nkeys - WINDOW - t_new, KV_W), F32)
    for sq in range(SB):
        t0 = sq * t_new
        kcat = jnp.concatenate([ck_ref[sq], k_new[t0:t0 + t_new], pad], axis=0)
        vcat = jnp.concatenate([cv_ref[sq], v_new[t0:t0 + t_new], pad], axis=0)
        kvar, vb = _kv_variants(kcat, vcat)
        q = za[t0:t0 + t_new, C_Q:C_K]
        outs = {}
        for h in range(KV_HEADS):
            qpair = jnp.concatenate([q[:, (2 * h) * LANES:(2 * h + 1) * LANES],
                                     q[:, (2 * h + 1) * LANES:(2 * h + 2) * LANES]], axis=0).astype(BF)
            for p in range(2):
                s = _dot_nt(qpair, kvar[h][p]) + bias_sc[2 * h + p]
                sink = jnp.where(upper1, sink_ref[h * GROUP + 2 + p], sink_ref[h * GROUP + p])
                o2 = _softmax_pv(s, sink, vb)
                outs[(h, p)] = o2[0:t_new]
                outs[(h, 2 + p)] = o2[t_new:2 * t_new]
        tiles = [jnp.where(lo, outs[(0, g)], outs[(1, g)]) for g in range(GROUP)]
        oatt_sc[pl.ds(r0 + t0, t_new), :] = jnp.concatenate(tiles, axis=1)

    bl_rows = jnp.concatenate([_row_bcast(b, sq * t_new + t_new - 1, t_new) for sq in range(SB)], axis=0)
    eb = jnp.exp(b)
    enb = jnp.exp(-b)
    ekend = jnp.exp(bl_rows - b)
    e_last = jnp.exp(bl_rows)
    gq = zg[:, 0:GLA_DK] * (GLA_HK ** -0.5)
    gk = zg[:, GLA_DK:2 * GLA_DK]
    qt = gq * eb
    ktb = (gk * enb).astype(BF)
    kend = gk * ekend
    ri = lax.broadcasted_iota(jnp.int32, (rows, rows), 0)
    ci = lax.broadcasted_iota(jnp.int32, (rows, rows), 1)
    tshift = t_new.bit_length() - 1
    causal = ((ri >> tshift) == (ci >> tshift)) & (ri >= ci)
    vrow = lax.broadcasted_iota(jnp.int32, (rows, GLA_HV), 0) >> tshift
    eye = (lax.broadcasted_iota(jnp.int32, (GLA_HK, GLA_HK), 0) == lax.broadcasted_iota(jnp.int32, (GLA_HK, GLA_HK), 1))
    for hh in range(GLA_HEADS):
        ks = slice(hh * GLA_HK, (hh + 1) * GLA_HK)
        vh = zg[:, 2 * GLA_DK + hh * GLA_HV:2 * GLA_DK + (hh + 1) * GLA_HV]
        vhb = vh.astype(BF)
        qth = qt[:, ks]
        a = jnp.where(causal, _dot_nt(qth.astype(BF), ktb[:, ks]), 0.0).astype(BF)
        o_intra = _dot(a, vhb)
        kend_t = kend[:, ks].T.astype(BF)
        o_seq = []
        for sq in range(SB):
            t0 = sq * t_new
            s_old = st_ref[sq, hh]
            o_seq.append(_dot(qth[t0:t0 + t_new].astype(BF), s_old.astype(BF)))
            upd = _dot(kend_t, jnp.where(vrow == sq, vh, 0.0).astype(BF))
            ns_ref[sq, hh] = _col_from_row(e_last[t0:t0 + 1, ks], eye) * s_old + upd
        o = jnp.concatenate(o_seq, axis=0) + o_intra
        ogla_sc[pl.ds(r0, rows), hh * GLA_HV:(hh + 1) * GLA_HV] = _gla_norm(o, gnw_ref)

    @pl.when(j == nsub - 1)
    def _emit():
        x = x_ref[...]
        zr = _dot(x.astype(BF), w_ref[:, C_RA:C_R_END])
        y_ref[...] = _out_stage(x, zr[:, :D_MODEL], zr[:, D_MODEL:], oatt_sc[...], za_sc[:, C_GA:C_ATT_END],
                                ogla_sc[...], zg_sc[:, 2 * GLA_DK + GLA_DV:], wpa_ref, wpg_ref, wo_ref,
                                lng_ref, lnb_ref, dn_alpha)


def _resident(shape):
    nd = len(shape)
    return pl.BlockSpec(shape, lambda *_: (0,) * nd, pipeline_mode=pl.Buffered(1))


def _smem():
    return pl.BlockSpec(memory_space=pltpu.SMEM)


def _pack_weights(w_in, w_gk_up, w_pa):
    perm = _att_perm()
    seg = np.cumsum((0, ATT_W, KV_W, KV_W, ATT_W, GLA_DK, GLA_DK, GLA_DV, GLA_DV, GLA_RANK, D_MODEL, D_MODEL))
    q, k, v, ga, gq, gk, gv, gg, lr, ra, rg = [w_in[:, seg[n]:seg[n + 1]] for n in range(11)]
    lr_pad = jnp.pad(lr, ((0, 0), (0, LANES - GLA_RANK)))
    w = jnp.concatenate([q * (HEAD_DIM ** -0.5), k, v, ga[:, perm], gq, gk, gv, gg, ra, rg, lr_pad], axis=1)
    wup = jnp.pad(w_gk_up, ((0, LANES - GLA_RANK), (0, 0)))
    return w.astype(BF), wup.astype(BF), w_pa[perm, :].astype(BF)


def _layer(xp, xs, cache_k, cache_v, state, table, sink, w_in, w_gk_up, b_gk, gla_norm_w, w_pa, w_pg, w_o,
           ln_g, ln_b, dn_alpha):
    assert HEAD_DIM ** -0.5 == 0.125
    t_prompt = xp.shape[0]
    n_seq, t_new = xs.shape[0], xs.shape[1]
    assert t_prompt % TB == 0 and (n_seq * t_new) % TB == 0 and TB % (SB * t_new) == 0
    assert t_new & (t_new - 1) == 0 and t_new <= GLA_CHUNK and t_new % 8 == 0

    w, wup, wpa = _pack_weights(w_in, w_gk_up, w_pa)
    wpg = w_pg.astype(BF)
    wo = w_o.astype(BF)
    bgk = b_gk.reshape(1, GLA_DK)
    gnw = gla_norm_w.reshape(1, GLA_HV)
    lng = ln_g.reshape(1, D_MODEL)
    lnb = ln_b.reshape(1, D_MODEL)
    table_flat = table.reshape(N_BUCKETS * N_HEADS)

    weights = (w, wup, bgk, wpa, wpg, wo, gnw, lng, lnb)
    weight_specs = [_resident(a.shape) for a in weights]

    bucket_p = jnp.asarray(_t5_bucket_np(np.arange(WINDOW)[:, None] + WINDOW - np.arange(2 * WINDOW)[None, :]))
    y_p, ko, vo, s_p = pl.pallas_call(
        functools.partial(_prompt_kernel, dn_alpha=dn_alpha),
        grid=(t_prompt // TB,),
        in_specs=[pl.BlockSpec((TB, D_MODEL), lambda s: (s, 0))] + weight_specs
                 + [_resident(bucket_p.shape), _smem(), _smem()],
        out_specs=[pl.BlockSpec((TB, D_MODEL), lambda s: (s, 0)),
                   pl.BlockSpec((WINDOW, KV_W), lambda s: (0, 0)),
                   pl.BlockSpec((WINDOW, KV_W), lambda s: (0, 0)),
                   pl.BlockSpec((GLA_HEADS, GLA_HK, GLA_HV), lambda s: (0, 0, 0))],
        out_shape=[jax.ShapeDtypeStruct((t_prompt, D_MODEL), F32),
                   jax.ShapeDtypeStruct((WINDOW, KV_W), F32),
                   jax.ShapeDtypeStruct((WINDOW, KV_W), F32),
                   jax.ShapeDtypeStruct((GLA_HEADS, GLA_HK, GLA_HV), F32)],
        scratch_shapes=[pltpu.VMEM((2, N_HEADS, WINDOW, 2 * WINDOW), F32),
                        pltpu.VMEM((2 * KV_HEADS, WINDOW, LANES), BF),
                        pltpu.VMEM((WINDOW, LANES), BF)],
        compiler_params=pltpu.CompilerParams(dimension_semantics=("arbitrary",), vmem_limit_bytes=VMEM_LIMIT),
        name="prompt_layer",
    )(xp, *weights, bucket_p, table_flat, sink)

    nsub = TB // (SB * t_new)
    dist_s = np.arange(t_new)[:, None] + WINDOW - np.arange(2 * WINDOW)[None, :]
    bucket_s = jnp.asarray(np.tile(_t5_bucket_np(dist_s), (2, 1)))
    ck = cache_k.reshape(n_seq, WINDOW, KV_W)
    cv = cache_v.reshape(n_seq, WINDOW, KV_W)
    xs2 = xs.reshape(n_seq * t_new, D_MODEL)
    seq_blk = lambda i, j: (i * nsub + j, 0, 0)
    y_s, nk, nv, ns = pl.pallas_call(
        functools.partial(_sample_kernel, dn_alpha=dn_alpha, t_new=t_new),
        grid=(n_seq * t_new // TB, nsub),
        in_specs=[pl.BlockSpec((TB, D_MODEL), lambda i, j: (i, 0))] + weight_specs
                 + [_resident(bucket_s.shape), _smem(), _smem(),
                    pl.BlockSpec((SB, WINDOW, KV_W), seq_blk),
                    pl.BlockSpec((SB, WINDOW, KV_W), seq_blk),
                    pl.BlockSpec((SB, GLA_HEADS, GLA_HK, GLA_HV), lambda i, j: (i * nsub + j, 0, 0, 0))],
        out_specs=[pl.BlockSpec((TB, D_MODEL), lambda i, j: (i, 0)),
                   pl.BlockSpec((SB, WINDOW, KV_W), seq_blk),
                   pl.BlockSpec((SB, WINDOW, KV_W), seq_blk),
                   pl.BlockSpec((SB, GLA_HEADS, GLA_HK, GLA_HV), lambda i, j: (i * nsub + j, 0, 0, 0))],
        out_shape=[jax.ShapeDtypeStruct((n_seq * t_new, D_MODEL), F32),
                   jax.ShapeDtypeStruct((n_seq, WINDOW, KV_W), F32),
                   jax.ShapeDtypeStruct((n_seq, WINDOW, KV_W), F32),
                   jax.ShapeDtypeStruct((n_seq, GLA_HEADS, GLA_HK, GLA_HV), F32)],
        scratch_shapes=[pltpu.VMEM((2 * KV_HEADS, 2 * t_new, 2 * WINDOW), F32),
                        pltpu.VMEM((TB, C_ATT_END), F32),
                        pltpu.VMEM((TB, C_GLA_END - C_GQ), F32),
                        pltpu.VMEM((TB, GLA_DK), F32),
                        pltpu.VMEM((TB, ATT_W), F32),
                        pltpu.VMEM((TB, GLA_DV), F32)],
        compiler_params=pltpu.CompilerParams(dimension_semantics=("arbitrary", "arbitrary"),
                                             vmem_limit_bytes=VMEM_LIMIT),
        name="sample_layer",
    )(xs2, *weights, bucket_s, table_flat, sink, ck, cv, state)

    return (y_p, y_s.reshape(n_seq, t_new, D_MODEL),
            ko.reshape(1, WINDOW, KV_HEADS, HEAD_DIM), vo.reshape(1, WINDOW, KV_HEADS, HEAD_DIM), s_p[None],
            nk.reshape(n_seq, WINDOW, KV_HEADS, HEAD_DIM), nv.reshape(n_seq, WINDOW, KV_HEADS, HEAD_DIM), ns)


def kernel(x_prompt, x_sample, cache_k, cache_v, state_gla, rel_bias_table, w_in, w_gk_up, b_gk, attn_sink,
           gla_norm_w, w_pa, w_pg, w_o, ln_g, ln_b):
    depth = w_in.shape[0]
    dn_alpha = (2.0 * depth) ** 0.25
    assert x_prompt.shape[0] == 1
    xp, xs = x_prompt[0], x_sample
    kp, vp, sp, ksm, vsm, ssm = [], [], [], [], [], []
    for l in range(depth):
        xp, xs, nkp, nvp, nsp, nks, nvs, nss = _layer(
            xp, xs, cache_k[l], cache_v[l], state_gla[l], rel_bias_table, attn_sink[l], w_in[l], w_gk_up[l],
            b_gk[l], gla_norm_w[l], w_pa[l], w_pg[l], w_o[l], ln_g[l], ln_b[l], dn_alpha)
        kp.append(nkp); vp.append(nvp); sp.append(nsp); ksm.append(nks); vsm.append(nvs); ssm.append(nss)
    return (xp[None], xs, jnp.stack(kp), jnp.stack(vp), jnp.stack(sp), jnp.stack(ksm), jnp.stack(vsm),
            jnp.stack(ssm))
```

```python
import functools
import math

import numpy as np
import jax
import jax.numpy as jnp
from jax import lax
from jax.experimental import pallas as pl
from jax.experimental.pallas import tpu as pltpu

D_MODEL = 1024
N_HEADS = 8
KV_HEADS = 2
HEAD_DIM = 64
GROUP = N_HEADS // KV_HEADS
ATT_W = N_HEADS * HEAD_DIM
KV_W = KV_HEADS * HEAD_DIM
WINDOW = 128
N_BUCKETS = 32
MAX_DISTANCE = 128
MAX_EXACT = N_BUCKETS // 2
GLA_HEADS = 4
GLA_DK = D_MODEL // 2
GLA_DV = D_MODEL
GLA_HK = GLA_DK // GLA_HEADS
GLA_HV = GLA_DV // GLA_HEADS
GLA_RANK = 16
GATE_NORMALIZER = 16.0
GLA_CHUNK = 64
EPS = 1e-5
NEG = -1e30

LANES = 128
TB = 256
NCHUNK = TB // GLA_CHUNK
NBLK = TB // WINDOW
SB = 8
VMEM_LIMIT = 56 * 1024 * 1024

A_Q, A_K, A_V, A_G, A_END = 0, ATT_W, ATT_W + KV_W, ATT_W + 2 * KV_W, 2 * ATT_W + 2 * KV_W
G_Q, G_K, G_V, G_G, G_END = 0, GLA_DK, 2 * GLA_DK, 2 * GLA_DK + GLA_DV, 2 * GLA_DK + 2 * GLA_DV

BF = jnp.bfloat16
F32 = jnp.float32


def _t5_bucket_np(dist):
    n = np.maximum(dist, 0)
    nf = np.maximum(n, 1).astype(np.float64)
    large = MAX_EXACT + (np.log(nf / MAX_EXACT) / math.log(MAX_DISTANCE / MAX_EXACT)
                         * (N_BUCKETS - MAX_EXACT)).astype(np.int32)
    large = np.minimum(large, N_BUCKETS - 1)
    return np.where(n < MAX_EXACT, n, large).astype(np.int32)


def _dot(a, b):
    return jnp.dot(a, b, preferred_element_type=F32)


def _dot_nt(a, b):
    return lax.dot_general(a, b, (((1,), (1,)), ((), ())), preferred_element_type=F32)


def _dot_tn(a, b):
    return lax.dot_general(a, b, (((0,), (0,)), ((), ())), preferred_element_type=F32)


def _log_sigmoid(x):
    return jnp.minimum(x, 0.0) - jnp.log1p(jnp.exp(-jnp.abs(x)))


def _sigmoid(x):
    return 1.0 / (1.0 + jnp.exp(-x))


def _group_cumsum(x, group):
    row = lax.broadcasted_iota(jnp.int32, x.shape, 0) & (group - 1)
    s = 1
    while s < group:
        x = x + jnp.where(row >= s, pltpu.roll(x, s, 0), 0.0)
        s *= 2
    return x


def _col_from_row(row, eye):
    n = row.shape[1]
    return jnp.sum(jnp.where(eye, jnp.broadcast_to(row, (n, n)), 0.0), axis=1, keepdims=True)


def _kv_variants(kk, vv):
    lo = lax.broadcasted_iota(jnp.int32, kk.shape, 1) < HEAD_DIM
    sw = pltpu.roll(kk, HEAD_DIM, 1)
    z = jnp.zeros_like(kk)
    kvar = ((jnp.where(lo, kk, z).astype(BF), jnp.where(lo, z, sw).astype(BF)),
            (jnp.where(lo, sw, z).astype(BF), jnp.where(lo, z, kk).astype(BF)))
    return kvar, vv.astype(BF)


def _build_bias(bucket, table_ref, mask, head):
    acc = jnp.zeros(bucket.shape, F32)
    for b in range(N_BUCKETS):
        acc = jnp.where(bucket == b, table_ref[b * N_HEADS + head], acc)
    return jnp.where(mask, acc, NEG)


def _sink_softmax(s, sink):
    m = jnp.maximum(jnp.max(s, axis=-1, keepdims=True), sink)
    p = jnp.exp(s - m)
    l = jnp.sum(p, axis=-1, keepdims=True) + jnp.exp(sink - m)
    return p.astype(BF), 1.0 / l


def _decay_prefix(zl, wup_ref, bgk_ref, group):
    pre = _dot(zl.astype(BF), wup_ref[...]) + bgk_ref[...]
    return _group_cumsum(_log_sigmoid(pre) / GATE_NORMALIZER, group)


def _gla_norm(o, gnw_ref):
    return o * lax.rsqrt(jnp.mean(o * o, axis=-1, keepdims=True) + EPS) * gnw_ref[...]


def _layer_norm(xf, lng_ref, lnb_ref):
    mu = jnp.mean(xf, axis=-1, keepdims=True)
    xc = xf - mu
    var = jnp.mean(xc * xc, axis=-1, keepdims=True)
    return xc * lax.rsqrt(var + EPS) * lng_ref[...] + lnb_ref[...]


def _merge_heads(outs, lo):
    return jnp.concatenate([jnp.where(lo, outs[(0, g)], outs[(1, g)]) for g in range(GROUP)], axis=1)


def _prompt_kernel(x_ref, watt_ref, wgla_ref, wr_ref, wlr_ref, wup_ref, bgk_ref, wpa_ref, wpg_ref, wo_ref,
                   gnw_ref, lng_ref, lnb_ref, bucket_ref, table_ref, sink_ref,
                   y_ref, ko_ref, vo_ref, s_ref,
                   bias_sc, kprev_sc, vprev_sc, *, dn_alpha):
    step = pl.program_id(0)
    nsteps = pl.num_programs(0)

    @pl.when(step == 0)
    def _init():
        bucket = bucket_ref[...]
        qi = lax.broadcasted_iota(jnp.int32, bucket.shape, 0)
        ki = lax.broadcasted_iota(jnp.int32, bucket.shape, 1)
        dist = qi + WINDOW - ki
        in_win = (dist >= 0) & (dist <= WINDOW)
        for n in range(N_HEADS):
            full = _build_bias(bucket, table_ref, in_win, n)
            bias_sc[0, n] = full
            bias_sc[1, n] = jnp.where(ki >= WINDOW, full, NEG)
        kprev_sc[...] = jnp.zeros(kprev_sc.shape, BF)
        vprev_sc[...] = jnp.zeros(vprev_sc.shape, BF)
        s_ref[...] = jnp.zeros(s_ref.shape, F32)

    x = x_ref[...]
    xb = x.astype(BF)

    za = _dot(xb, watt_ref[...])
    qb = za[:, A_Q:A_K].astype(BF)
    k_blk = [tuple(tuple(kprev_sc[2 * h + p] for p in range(2)) for h in range(KV_HEADS))]
    v_blk = [vprev_sc[...]]
    for blk in range(NBLK):
        r0 = blk * WINDOW
        kv, vv = _kv_variants(za[r0:r0 + WINDOW, A_K:A_V], za[r0:r0 + WINDOW, A_V:A_G])
        k_blk.append(kv)
        v_blk.append(vv)
    for h in range(KV_HEADS):
        for p in range(2):
            kprev_sc[2 * h + p] = k_blk[NBLK][h][p]
    vprev_sc[...] = v_blk[NBLK]
    scores = {}
    for blk in range(NBLK):
        r0 = blk * WINDOW
        for h in range(KV_HEADS):
            qpair = jnp.concatenate([qb[r0:r0 + WINDOW, (2 * h) * LANES:(2 * h + 1) * LANES],
                                     qb[r0:r0 + WINDOW, (2 * h + 1) * LANES:(2 * h + 2) * LANES]], axis=0)
            for p in range(2):
                kcat = jnp.concatenate([k_blk[blk][h][p], k_blk[blk + 1][h][p]], axis=0)
                scores[(blk, h, p)] = _dot_nt(qpair, kcat)

    @pl.when(step == nsteps - 1)
    def _emit_kv():
        ko_ref[...] = za[TB - WINDOW:, A_K:A_V]
        vo_ref[...] = za[TB - WINDOW:, A_V:A_G]

    zl = _dot(xb, wlr_ref[...])
    zg = _dot(xb, wgla_ref[...])
    probs = {}
    for blk in range(NBLK):
        first = jnp.where(step == 0, 1, 0) if blk == 0 else 0
        for h in range(KV_HEADS):
            for p in range(2):
                parts = []
                for half in range(2):
                    n = h * GROUP + 2 * half + p
                    sh = scores[(blk, h, p)][half * WINDOW:(half + 1) * WINDOW] + bias_sc[first, n]
                    parts.append(_sink_softmax(sh, sink_ref[n]))
                probs[(blk, h, p)] = (jnp.concatenate([parts[0][0], parts[1][0]], axis=0),
                                      jnp.concatenate([parts[0][1], parts[1][1]], axis=0))
    b = _decay_prefix(zl, wup_ref, bgk_ref, GLA_CHUNK)

    lo = lax.broadcasted_iota(jnp.int32, (WINDOW, LANES), 1) < HEAD_DIM
    o_rows = []
    for blk in range(NBLK):
        vcat = jnp.concatenate([v_blk[blk], v_blk[blk + 1]], axis=0)
        outs = {}
        for h in range(KV_HEADS):
            for p in range(2):
                pb, linv = probs[(blk, h, p)]
                o2 = _dot(pb, vcat) * linv
                outs[(h, p)] = o2[:WINDOW]
                outs[(h, 2 + p)] = o2[WINDOW:]
        o_rows.append(_merge_heads(outs, lo))
    zr = _dot(xb, wr_ref[...])
    o_att = jnp.concatenate(o_rows, axis=0)
    g_att = za[:, A_G:A_END]
    att_in = (o_att * (g_att * _sigmoid(g_att))).astype(BF)

    bl = [b[c * GLA_CHUNK + GLA_CHUNK - 1:(c + 1) * GLA_CHUNK, :] for c in range(NCHUNK)]

    def rows_of(parts):
        return jnp.concatenate([jnp.broadcast_to(p_, (GLA_CHUNK, GLA_DK)) for p_ in parts], axis=0)

    def span(lo_c, hi_c):
        tot = jnp.zeros((1, GLA_DK), F32)
        for i in range(lo_c, hi_c):
            tot = tot + bl[i]
        return tot

    eb = jnp.exp(b)
    enb = jnp.exp(-b)
    ekend = jnp.exp(rows_of(bl) - b)
    e_pre = rows_of([jnp.exp(span(0, c)) for c in range(NCHUNK)])
    e_tail = rows_of([jnp.exp(span(c + 1, NCHUNK)) for c in range(NCHUNK)])
    e_tot = jnp.exp(span(0, NCHUNK))
    gq = zg[:, G_Q:G_K] * (GLA_HK ** -0.5)
    gk = zg[:, G_K:G_V]
    qt = gq * eb
    kend = gk * ekend
    qp = (qt * e_pre).astype(BF)
    kfull = (kend * e_tail).astype(BF)
    qtb = qt.astype(BF)
    ktb = (gk * enb).astype(BF)
    kmid = {}
    for c in range(NCHUNK):
        for j in range(c):
            kj = kend[j * GLA_CHUNK:(j + 1) * GLA_CHUNK]
            kmid[(c, j)] = (kj if c == j + 1 else kj * jnp.exp(span(j + 1, c))).astype(BF)
    tril = (lax.broadcasted_iota(jnp.int32, (TB, TB), 0) >= lax.broadcasted_iota(jnp.int32, (TB, TB), 1))
    eye = (lax.broadcasted_iota(jnp.int32, (GLA_HK, GLA_HK), 0) == lax.broadcasted_iota(jnp.int32, (GLA_HK, GLA_HK), 1))
    zero_k = jnp.zeros((GLA_CHUNK, GLA_HK), BF)
    vh, o_inter, a_raw = [], [], []
    for hh in range(GLA_HEADS):
        ks = slice(hh * GLA_HK, (hh + 1) * GLA_HK)
        vh.append(zg[:, G_V + hh * GLA_HV:G_V + (hh + 1) * GLA_HV].astype(BF))
        s_old = s_ref[hh]
        o_inter.append(_dot(qp[:, ks], s_old.astype(BF)))
        a_rows = []
        for c in range(NCHUNK):
            keys = [kmid[(c, j)][:, ks] for j in range(c)] + [ktb[c * GLA_CHUNK:(c + 1) * GLA_CHUNK, ks]]
            keys += [zero_k] * (NCHUNK - 1 - c)
            a_rows.append(_dot_nt(qtb[c * GLA_CHUNK:(c + 1) * GLA_CHUNK, ks], jnp.concatenate(keys, axis=0)))
        a_raw.append(jnp.concatenate(a_rows, axis=0))
        s_ref[hh] = _col_from_row(e_tot[:, ks], eye) * s_old + _dot_tn(kfull[:, ks], vh[hh])
    h_att = _dot(att_in, wpa_ref[...])
    o_heads = []
    for hh in range(GLA_HEADS):
        a = jnp.where(tril, a_raw[hh], 0.0).astype(BF)
        o_heads.append(_gla_norm(o_inter[hh] + _dot(a, vh[hh]), gnw_ref))
    o_gla = jnp.concatenate(o_heads, axis=1)
    g_gla = zg[:, G_G:G_END]
    h_gla = _dot((o_gla * (g_gla * _sigmoid(g_gla))).astype(BF), wpg_ref[...])

    merged = _sigmoid(zr[:, :D_MODEL]) * h_att + _sigmoid(zr[:, D_MODEL:]) * h_gla
    y = _dot(merged.astype(BF), wo_ref[...])
    y_ref[...] = _layer_norm(dn_alpha * x + y, lng_ref, lnb_ref)


def _sample_kernel(x_ref, watt_ref, wgla_ref, wr_ref, wlr_ref, wup_ref, bgk_ref, wpa_ref, wpg_ref, wo_ref,
                   gnw_ref, lng_ref, lnb_ref, bucket_ref, table_ref, sink_ref, ck_ref, cv_ref, st_ref,
                   y_ref, nk_ref, nv_ref, ns_ref,
                   bias_sc, sink_sc, za_sc, zg_sc, b_sc, oatt_sc, ogla_sc, *, dn_alpha, t_new):
    i = pl.program_id(0)
    j = pl.program_id(1)
    nsub = pl.num_programs(1)
    rows = SB * t_new
    nkeys = 2 * WINDOW
    tshift = t_new.bit_length() - 1
    ntile = 2 * KV_HEADS

    @pl.when((i == 0) & (j == 0))
    def _init():
        bucket = bucket_ref[...]
        ri = lax.broadcasted_iota(jnp.int32, bucket.shape, 0)
        ki = lax.broadcasted_iota(jnp.int32, bucket.shape, 1)
        dist = (ri & (t_new - 1)) + WINDOW - ki
        in_win = (dist >= 0) & (dist <= WINDOW)
        rc = lax.broadcasted_iota(jnp.int32, (ntile * 2 * t_new, 1), 0)
        acc = jnp.zeros(bucket.shape, F32)
        sk = jnp.zeros(rc.shape, F32)
        for h in range(KV_HEADS):
            for p in range(2):
                for half in range(2):
                    n = h * GROUP + 2 * half + p
                    r_lo = ((2 * h + p) * 2 + half) * t_new
                    acc = jnp.where((ri >= r_lo) & (ri < r_lo + t_new), _build_bias(bucket, table_ref, in_win, n), acc)
                    sk = jnp.where((rc >= r_lo) & (rc < r_lo + t_new), sink_ref[n], sk)
        bias_sc[...] = acc
        sink_sc[...] = sk

    @pl.when(j == 0)
    def _project():
        xb = x_ref[...].astype(BF)
        za_sc[...] = _dot(xb, watt_ref[...])
        zg_sc[...] = _dot(xb, wgla_ref[...])
        b_sc[...] = _decay_prefix(_dot(xb, wlr_ref[...]), wup_ref, bgk_ref, t_new)

    r0 = pl.multiple_of(j * rows, rows)
    za = za_sc[pl.ds(r0, rows), :]
    zg = zg_sc[pl.ds(r0, rows), :]
    b = b_sc[pl.ds(r0, rows), :]

    k_new = za[:, A_K:A_V]
    v_new = za[:, A_V:A_G]
    nk_ref[:, 0:WINDOW - t_new, :] = ck_ref[:, t_new:WINDOW, :]
    nv_ref[:, 0:WINDOW - t_new, :] = cv_ref[:, t_new:WINDOW, :]
    nk_ref[:, WINDOW - t_new:WINDOW, :] = k_new.reshape(SB, t_new, KV_W)
    nv_ref[:, WINDOW - t_new:WINDOW, :] = v_new.reshape(SB, t_new, KV_W)
    pad = jnp.zeros((nkeys - WINDOW - t_new, KV_W), F32)
    scores, vbs = [], []
    for sq in range(SB):
        t0 = sq * t_new
        kcat = jnp.concatenate([ck_ref[sq], k_new[t0:t0 + t_new], pad], axis=0)
        vcat = jnp.concatenate([cv_ref[sq], v_new[t0:t0 + t_new], pad], axis=0)
        kvar, vb = _kv_variants(kcat, vcat)
        vbs.append(vb)
        q = za[t0:t0 + t_new, A_Q:A_K]
        tiles = []
        for h in range(KV_HEADS):
            qpair = jnp.concatenate([q[:, (2 * h) * LANES:(2 * h + 1) * LANES],
                                     q[:, (2 * h + 1) * LANES:(2 * h + 2) * LANES]], axis=0).astype(BF)
            for p in range(2):
                tiles.append(_dot_nt(qpair, kvar[h][p]))
        scores.append(jnp.concatenate(tiles, axis=0))

    bl_rows = jnp.concatenate([jnp.broadcast_to(b[sq * t_new + t_new - 1:(sq + 1) * t_new, :], (t_new, GLA_DK))
                               for sq in range(SB)], axis=0)
    eb = jnp.exp(b)
    enb = jnp.exp(-b)
    ekend = jnp.exp(bl_rows - b)
    e_last = jnp.exp(bl_rows)
    gq = zg[:, G_Q:G_K] * (GLA_HK ** -0.5)
    gk = zg[:, G_K:G_V]
    qt = gq * eb
    ktb = (gk * enb).astype(BF)
    kend = gk * ekend
    ri = lax.broadcasted_iota(jnp.int32, (rows, rows), 0)
    ci = lax.broadcasted_iota(jnp.int32, (rows, rows), 1)
    causal = ((ri >> tshift) == (ci >> tshift)) & (ri >= ci)
    vrow = lax.broadcasted_iota(jnp.int32, (rows, GLA_HV), 0) >> tshift
    eye = (lax.broadcasted_iota(jnp.int32, (GLA_HK, GLA_HK), 0) == lax.broadcasted_iota(jnp.int32, (GLA_HK, GLA_HK), 1))
    a_raw, vhb, o_inter = [], [], []
    for hh in range(GLA_HEADS):
        ks = slice(hh * GLA_HK, (hh + 1) * GLA_HK)
        vh = zg[:, G_V + hh * GLA_HV:G_V + (hh + 1) * GLA_HV]
        vhb.append(vh.astype(BF))
        qth = qt[:, ks]
        a_raw.append(_dot_nt(qth.astype(BF), ktb[:, ks]))
        kend_t = kend[:, ks].T.astype(BF)
        o_seq = []
        for sq in range(SB):
            t0 = sq * t_new
            s_old = st_ref[sq, hh]
            o_seq.append(_dot(qth[t0:t0 + t_new].astype(BF), s_old.astype(BF)))
            upd = _dot(kend_t, jnp.where(vrow == sq, vh, 0.0).astype(BF))
            ns_ref[sq, hh] = _col_from_row(e_last[t0:t0 + 1, ks], eye) * s_old + upd
        o_inter.append(jnp.concatenate(o_seq, axis=0))

    lo = lax.broadcasted_iota(jnp.int32, (t_new, LANES), 1) < HEAD_DIM
    probs = [_sink_softmax(scores[sq] + bias_sc[...], sink_sc[...]) for sq in range(SB)]
    for sq in range(SB):
        pb, linv = probs[sq]
        o2 = _dot(pb, vbs[sq]) * linv
        outs = {}
        for h in range(KV_HEADS):
            for p in range(2):
                for half in range(2):
                    r_lo = ((2 * h + p) * 2 + half) * t_new
                    outs[(h, 2 * half + p)] = o2[r_lo:r_lo + t_new]
        oatt_sc[pl.ds(r0 + sq * t_new, t_new), :] = _merge_heads(outs, lo)

    for hh in range(GLA_HEADS):
        a = jnp.where(causal, a_raw[hh], 0.0).astype(BF)
        o = o_inter[hh] + _dot(a, vhb[hh])
        ogla_sc[pl.ds(r0, rows), hh * GLA_HV:(hh + 1) * GLA_HV] = _gla_norm(o, gnw_ref)

    @pl.when(j == nsub - 1)
    def _emit():
        x = x_ref[...]
        zr = _dot(x.astype(BF), wr_ref[...])
        g_att = za_sc[:, A_G:A_END]
        g_gla = zg_sc[:, G_G:G_END]
        h_att = _dot((oatt_sc[...] * (g_att * _sigmoid(g_att))).astype(BF), wpa_ref[...])
        h_gla = _dot((ogla_sc[...] * (g_gla * _sigmoid(g_gla))).astype(BF), wpg_ref[...])
        merged = _sigmoid(zr[:, :D_MODEL]) * h_att + _sigmoid(zr[:, D_MODEL:]) * h_gla
        y = _dot(merged.astype(BF), wo_ref[...])
        y_ref[...] = _layer_norm(dn_alpha * x + y, lng_ref, lnb_ref)


def _resident(shape):
    nd = len(shape)
    return pl.BlockSpec(shape, lambda *_: (0,) * nd, pipeline_mode=pl.Buffered(1))


def _smem():
    return pl.BlockSpec(memory_space=pltpu.SMEM)


def _regroup_heads(a, axis):
    shp = a.shape
    a = a.reshape(shp[:axis] + (KV_HEADS, GROUP, HEAD_DIM) + shp[axis + 1:])
    return jnp.swapaxes(a, axis, axis + 1).reshape(shp)


def _pack_weights(w_in, w_gk_up, w_pa):
    seg = np.cumsum((0, ATT_W, KV_W, KV_W, ATT_W, GLA_DK, GLA_DK, GLA_DV, GLA_DV, GLA_RANK, D_MODEL, D_MODEL))
    q, k, v, ga, gq, gk, gv, gg, lr, ra, rg = [w_in[:, seg[n]:seg[n + 1]] for n in range(11)]
    w_att = jnp.concatenate([q * (HEAD_DIM ** -0.5), k, v, _regroup_heads(ga, 1)], axis=1).astype(BF)
    w_gla = w_in[:, seg[4]:seg[8]].astype(BF)
    w_r = w_in[:, seg[9]:seg[11]].astype(BF)
    w_lr = jnp.pad(lr, ((0, 0), (0, LANES - GLA_RANK))).astype(BF)
    w_up = jnp.pad(w_gk_up, ((0, LANES - GLA_RANK), (0, 0))).astype(BF)
    return w_att, w_gla, w_r, w_lr, w_up, _regroup_heads(w_pa, 0).astype(BF)


def _layer(xp, xs, cache_k, cache_v, state, table, sink, w_in, w_gk_up, b_gk, gla_norm_w, w_pa, w_pg, w_o,
           ln_g, ln_b, dn_alpha):
    assert HEAD_DIM ** -0.5 == 0.125
    t_prompt = xp.shape[0]
    n_seq, t_new = xs.shape[0], xs.shape[1]
    assert t_prompt % TB == 0 and (n_seq * t_new) % TB == 0 and TB % (SB * t_new) == 0
    assert t_new & (t_new - 1) == 0 and t_new <= GLA_CHUNK and t_new % 8 == 0

    w_att, w_gla, w_r, w_lr, w_up, wpa = _pack_weights(w_in, w_gk_up, w_pa)
    weights = (w_att, w_gla, w_r, w_lr, w_up, b_gk.reshape(1, GLA_DK), wpa, w_pg.astype(BF), w_o.astype(BF),
               gla_norm_w.reshape(1, GLA_HV), ln_g.reshape(1, D_MODEL), ln_b.reshape(1, D_MODEL))
    weight_specs = [_resident(a.shape) for a in weights]
    table_flat = table.reshape(N_BUCKETS * N_HEADS)

    bucket_p = jnp.asarray(_t5_bucket_np(np.arange(WINDOW)[:, None] + WINDOW - np.arange(2 * WINDOW)[None, :]))
    y_p, ko, vo, s_p = pl.pallas_call(
        functools.partial(_prompt_kernel, dn_alpha=dn_alpha),
        grid=(t_prompt // TB,),
        in_specs=[pl.BlockSpec((TB, D_MODEL), lambda s: (s, 0))] + weight_specs
                 + [_resident(bucket_p.shape), _smem(), _smem()],
        out_specs=[pl.BlockSpec((TB, D_MODEL), lambda s: (s, 0)),
                   pl.BlockSpec((WINDOW, KV_W), lambda s: (0, 0)),
                   pl.BlockSpec((WINDOW, KV_W), lambda s: (0, 0)),
                   pl.BlockSpec((GLA_HEADS, GLA_HK, GLA_HV), lambda s: (0, 0, 0))],
        out_shape=[jax.ShapeDtypeStruct((t_prompt, D_MODEL), F32),
                   jax.ShapeDtypeStruct((WINDOW, KV_W), F32),
                   jax.ShapeDtypeStruct((WINDOW, KV_W), F32),
                   jax.ShapeDtypeStruct((GLA_HEADS, GLA_HK, GLA_HV), F32)],
        scratch_shapes=[pltpu.VMEM((2, N_HEADS, WINDOW, 2 * WINDOW), F32),
                        pltpu.VMEM((2 * KV_HEADS, WINDOW, LANES), BF),
                        pltpu.VMEM((WINDOW, LANES), BF)],
        compiler_params=pltpu.CompilerParams(dimension_semantics=("arbitrary",), vmem_limit_bytes=VMEM_LIMIT),
        name="prompt_layer",
    )(xp, *weights, bucket_p, table_flat, sink)

    nsub = TB // (SB * t_new)
    dist_s = np.arange(t_new)[:, None] + WINDOW - np.arange(2 * WINDOW)[None, :]
    bucket_s = jnp.asarray(np.tile(_t5_bucket_np(dist_s), (4 * KV_HEADS, 1)))
    ck = cache_k.reshape(n_seq, WINDOW, KV_W)
    cv = cache_v.reshape(n_seq, WINDOW, KV_W)
    xs2 = xs.reshape(n_seq * t_new, D_MODEL)
    seq_blk = lambda i, j: (i * nsub + j, 0, 0)
    y_s, nk, nv, ns = pl.pallas_call(
        functools.partial(_sample_kernel, dn_alpha=dn_alpha, t_new=t_new),
        grid=(n_seq * t_new // TB, nsub),
        in_specs=[pl.BlockSpec((TB, D_MODEL), lambda i, j: (i, 0))] + weight_specs
                 + [_resident(bucket_s.shape), _smem(), _smem(),
                    pl.BlockSpec((SB, WINDOW, KV_W), seq_blk),
                    pl.BlockSpec((SB, WINDOW, KV_W), seq_blk),
                    pl.BlockSpec((SB, GLA_HEADS, GLA_HK, GLA_HV), lambda i, j: (i * nsub + j, 0, 0, 0))],
        out_specs=[pl.BlockSpec((TB, D_MODEL), lambda i, j: (i, 0)),
                   pl.BlockSpec((SB, WINDOW, KV_W), seq_blk),
                   pl.BlockSpec((SB, WINDOW, KV_W), seq_blk),
                   pl.BlockSpec((SB, GLA_HEADS, GLA_HK, GLA_HV), lambda i, j: (i * nsub + j, 0, 0, 0))],
        out_shape=[jax.ShapeDtypeStruct((n_seq * t_new, D_MODEL), F32),
                   jax.ShapeDtypeStruct((n_seq, WINDOW, KV_W), F32),
                   jax.ShapeDtypeStruct((n_seq, WINDOW, KV_W), F32),
                   jax.ShapeDtypeStruct((n_seq, GLA_HEADS, GLA_HK, GLA_HV), F32)],
        scratch_shapes=[pltpu.VMEM((4 * KV_HEADS * t_new, 2 * WINDOW), F32),
                        pltpu.VMEM((4 * KV_HEADS * t_new, 1), F32),
                        pltpu.VMEM((TB, A_END), F32),
                        pltpu.VMEM((TB, G_END), F32),
                        pltpu.VMEM((TB, GLA_DK), F32),
                        pltpu.VMEM((TB, ATT_W), F32),
                        pltpu.VMEM((TB, GLA_DV), F32)],
        compiler_params=pltpu.CompilerParams(dimension_semantics=("arbitrary", "arbitrary"),
                                             vmem_limit_bytes=VMEM_LIMIT),
        name="sample_layer",
    )(xs2, *weights, bucket_s, table_flat, sink, ck, cv, state)

    return (y_p, y_s.reshape(n_seq, t_new, D_MODEL),
            ko.reshape(1, WINDOW, KV_HEADS, HEAD_DIM), vo.reshape(1, WINDOW, KV_HEADS, HEAD_DIM), s_p[None],
            nk.reshape(n_seq, WINDOW, KV_HEADS, HEAD_DIM), nv.reshape(n_seq, WINDOW, KV_HEADS, HEAD_DIM), ns)


def kernel(x_prompt, x_sample, cache_k, cache_v, state_gla, rel_bias_table, w_in, w_gk_up, b_gk, attn_sink,
           gla_norm_w, w_pa, w_pg, w_o, ln_g, ln_b):
    depth = w_in.shape[0]
    dn_alpha = (2.0 * depth) ** 0.25
    assert x_prompt.shape[0] == 1
    xp, xs = x_prompt[0], x_sample
    kp, vp, sp, ksm, vsm, ssm = [], [], [], [], [], []
    for l in range(depth):
        xp, xs, nkp, nvp, nsp, nks, nvs, nss = _layer(
            xp, xs, cache_k[l], cache_v[l], state_gla[l], rel_bias_table, attn_sink[l], w_in[l], w_gk_up[l],
            b_gk[l], gla_norm_w[l], w_pa[l], w_pg[l], w_o[l], ln_g[l], ln_b[l], dn_alpha)
        kp.append(nkp); vp.append(nvp); sp.append(nsp); ksm.append(nks); vsm.append(nvs); ssm.append(nss)
    return (xp[None], xs, jnp.stack(kp), jnp.stack(vp), jnp.stack(sp), jnp.stack(ksm), jnp.stack(vsm),
            jnp.stack(ssm))
```
